```python
import math
import jax, jax.numpy as jnp
from jax import lax
import numpy as np

D_MODEL = 1024
BATCH = 2
SEQ = 8192
DEPTH = 2
DEC_BATCH = 32
DEC_SEQ = 8
PAST_LEN = 8192
PAGE_SIZE = 128

N_MIXERS = 2
N_NSA_LAYERS = (DEPTH + 1) // 2
N_DIFF_LAYERS = DEPTH // 2
NSA_HEADS = 16
NSA_KV_HEADS = 2
NSA_HEAD_DIM = D_MODEL // NSA_HEADS
NSA_GROUP = NSA_HEADS // NSA_KV_HEADS
CMP_BLOCK = 32
CMP_STRIDE = 16
CMP_HIDDEN = 4 * NSA_HEAD_DIM
SEL_BLOCK = 64
SEL_TOPK = 16
WINDOW = 512
NSA_IN = NSA_HEADS * NSA_HEAD_DIM + 6 * NSA_KV_HEADS * NSA_HEAD_DIM + 3 * NSA_HEADS
DIFF_HEADS = 8
DIFF_DIM = D_MODEL // (2 * DIFF_HEADS)
DIFF_IN = 3 * D_MODEL
D_FF = 4 * D_MODEL
Q_BLOCK = 128
ROPE_THETA = 10000.0
EPS = 1e-6
NEG = -1e30
FORCE = 1e6

kernel_name = "nsa_diffattn_hybrid_step"


def rms_norm(x, g):
    xf = x.astype(jnp.float32)
    y = xf * lax.rsqrt(jnp.mean(xf * xf, axis=-1, keepdims=True) + EPS)
    return (y * g.astype(jnp.float32)).astype(x.dtype)


def rope(x, pos):
    half = x.shape[-1] // 2
    inv = ROPE_THETA ** (-jnp.arange(half, dtype=jnp.float32) / half)
    ang = pos.astype(jnp.float32)[:, None] * inv
    ang = ang.reshape(ang.shape[:1] + (1,) * (x.ndim - 3) + (half,))
    cos, sin = jnp.cos(ang), jnp.sin(ang)
    xf = x.astype(jnp.float32)
    x1, x2 = xf[..., :half], xf[..., half:]
    return jnp.concatenate([x1 * cos - x2 * sin, x2 * cos + x1 * sin], axis=-1).astype(x.dtype)


def gather_pages(cache, page_table):
    pages = cache[page_table]
    return pages.reshape((page_table.shape[0], -1) + cache.shape[2:])


def pad_rows(x, n):
    return jnp.pad(x, ((0, 0), (0, n)) + ((0, 0),) * (x.ndim - 2))


def sq_relu_mlp(h, w_up, w_down):
    u = jax.nn.relu(h @ w_up)
    return (u * u) @ w_down


def nsa_project(h, w_in, pos):
    B, T, _ = h.shape
    H, Hk, dh = NSA_HEADS, NSA_KV_HEADS, NSA_HEAD_DIM
    z = h @ w_in
    nq, nkv = H * dh, 6 * Hk * dh
    q = rope(z[..., :nq].reshape(B, T, H, dh), pos)
    kv = z[..., nq:nq + nkv].reshape(B, T, 3, 2, Hk, dh)
    gates = jax.nn.sigmoid(z[..., nq + nkv:].astype(jnp.float32)).reshape(B, T, H, 3)
    cmp_kv = kv[:, :, 0]
    sel_kv = jnp.stack([rope(kv[:, :, 1, 0], pos), kv[:, :, 1, 1]], axis=2)
    win_kv = jnp.stack([rope(kv[:, :, 2, 0], pos), kv[:, :, 2, 1]], axis=2)
    return q, gates, cmp_kv, sel_kv, win_kv


def compress(rows, pe, w1, w2):
    B, T, Hk, dh = rows.shape
    r = CMP_BLOCK // CMP_STRIDE
    chunks = rows.reshape(B, T // CMP_STRIDE, CMP_STRIDE, Hk, dh)
    n_cmp = T // CMP_STRIDE - r + 1
    blocks = jnp.concatenate([chunks[:, s:s + n_cmp] for s in range(r)], axis=2)
    blocks = blocks + pe[:, None, :]
    flat = blocks.transpose(0, 1, 3, 2, 4).reshape(B, n_cmp, Hk, CMP_BLOCK * dh)
    return jax.nn.silu(flat @ w1) @ w2


def nsa_summaries(cmp_kv, sel_kv, pe, w1, w2):
    B, T, _, Hk, dh = cmp_kv.shape
    kc = compress(cmp_kv[:, :, 0], pe[0], w1[0], w2[0])
    vc = compress(cmp_kv[:, :, 1], pe[1], w1[1], w2[1])
    kc = rope(kc, jnp.arange(kc.shape[1]) * CMP_STRIDE)
    ks_blk = sel_kv[:, :, 0].reshape(B, T // SEL_BLOCK, SEL_BLOCK, Hk, dh)
    vs_blk = sel_kv[:, :, 1].reshape(B, T // SEL_BLOCK, SEL_BLOCK, Hk, dh)
    return kc, vc, ks_blk, vs_blk


def nsa_branches(q, q_pos, gates, kc, vc, ks_blk, vs_blk, kw, vw, w_pos):
    f32 = jnp.float32
    B, Tq, H, dh = q.shape
    Hk = kc.shape[2]
    G = H // Hk
    qg = q.reshape(B, Tq, Hk, G, dh).astype(f32) * (dh ** -0.5)

    n_cmp = kc.shape[1]
    c_end = jnp.arange(n_cmp) * CMP_STRIDE + CMP_BLOCK - 1
    c_ok = c_end[None, :] <= q_pos[:, None]
    s = jnp.einsum('btkgd,bckd->bkgtc', qg, kc.astype(f32))
    p_cmp = jax.nn.softmax(jnp.where(c_ok, s, NEG), axis=-1)
    p_cmp = jnp.where(jnp.any(c_ok, axis=-1)[:, None], p_cmp, 0.0)
    o_cmp = jnp.einsum('bkgtc,bckd->btkgd', p_cmp, vc.astype(f32))

    n_sel = ks_blk.shape[1]
    ci = jnp.arange(n_cmp)[:, None]
    sj = jnp.arange(n_sel)[None, :]
    overlap = ((ci * CMP_STRIDE < (sj + 1) * SEL_BLOCK) & (ci * CMP_STRIDE + CMP_BLOCK > sj * SEL_BLOCK)).astype(f32)
    imp = jnp.einsum('bkgtc,cs->bkts', p_cmp, overlap)
    cur = q_pos // SEL_BLOCK
    j = jnp.arange(n_sel)[None, :]
    forced = (j == 0) | (j == cur[:, None]) | (j == cur[:, None] - 1)
    causal_blk = j * SEL_BLOCK <= q_pos[:, None]
    imp = jnp.where(forced, FORCE, jnp.where(causal_blk, imp, -1.0))
    _, idx = lax.top_k(imp, min(SEL_TOPK, n_sel))
    ksb = ks_blk.transpose(0, 3, 1, 2, 4)
    vsb = vs_blk.transpose(0, 3, 1, 2, 4)
    bi = jnp.arange(B)[:, None, None, None]
    hi = jnp.arange(Hk)[None, :, None, None]
    kg = ksb[bi, hi, idx].astype(f32)
    vg = vsb[bi, hi, idx].astype(f32)
    key_pos = idx[..., None] * SEL_BLOCK + jnp.arange(SEL_BLOCK)
    ok = key_pos <= q_pos[:, None, None]
    s = jnp.einsum('btkgd,bktnjd->bkgtnj', qg, kg)
    s = jnp.where(ok[:, :, None], s, NEG)
    p = jax.nn.softmax(s.reshape(s.shape[:4] + (-1,)), axis=-1).reshape(s.shape)
    o_sel = jnp.einsum('bkgtnj,bktnjd->btkgd', p, vg)

    ok_w = (w_pos[None, :] <= q_pos[:, None]) & (w_pos[None, :] > q_pos[:, None] - WINDOW) & (w_pos[None, :] >= 0)
    s = jnp.einsum('btkgd,bckd->bkgtc', qg, kw.astype(f32))
    p = jax.nn.softmax(jnp.where(ok_w, s, NEG), axis=-1)
    o_win = jnp.einsum('bkgtc,bckd->btkgd', p, vw.astype(f32))

    g = gates.reshape(B, Tq, Hk, G, 3)
    out = o_cmp * g[..., 0:1] + o_sel * g[..., 1:2] + o_win * g[..., 2:3]
    return out.reshape(B, Tq, H * dh)


def nsa_prompt(h, w_in, pe, w1, w2, w_out):
    B, T, _ = h.shape
    pos = jnp.arange(T)
    q, gates, cmp_kv, sel_kv, win_kv = nsa_project(h, w_in, pos)
    kc, vc, ks_blk, vs_blk = nsa_summaries(cmp_kv, sel_kv, pe, w1, w2)
    win_pad = jnp.pad(win_kv, ((0, 0), (WINDOW, 0), (0, 0), (0, 0), (0, 0)))

    def block(i):
        q0 = i * Q_BLOCK
        qb = lax.dynamic_slice_in_dim(q, q0, Q_BLOCK, axis=1)
        gb = lax.dynamic_slice_in_dim(gates, q0, Q_BLOCK, axis=1)
        wb = lax.dynamic_slice_in_dim(win_pad, q0, WINDOW + Q_BLOCK, axis=1)
        q_pos = q0 + jnp.arange(Q_BLOCK)
        w_pos = q0 - WINDOW + jnp.arange(WINDOW + Q_BLOCK)
        return nsa_branches(qb, q_pos, gb, kc, vc, ks_blk, vs_blk, wb[:, :, 0], wb[:, :, 1], w_pos)

    o = lax.map(block, jnp.arange(T // Q_BLOCK))
    o = o.transpose(1, 0, 2, 3).reshape(B, T, -1).astype(h.dtype)
    return o @ w_out, cmp_kv, sel_kv, win_kv[:, T - min(WINDOW, T):]


def nsa_sample(h, cache_cmp, cache_sel, win_buf, page_table, w_in, pe, w1, w2, w_out):
    B, Tn, _ = h.shape
    past = page_table.shape[1] * cache_cmp.shape[1]
    pos = past + jnp.arange(Tn)
    q, gates, cmp_kv, sel_kv, win_kv = nsa_project(h, w_in, pos)
    t_full = past + Tn
    n_pad = -(-t_full // SEL_BLOCK) * SEL_BLOCK - t_full
    full_cmp = pad_rows(jnp.concatenate([gather_pages(cache_cmp, page_table), cmp_kv], axis=1), n_pad)
    full_sel = pad_rows(jnp.concatenate([gather_pages(cache_sel, page_table), sel_kv], axis=1), n_pad)
    kc, vc, ks_blk, vs_blk = nsa_summaries(full_cmp, full_sel, pe, w1, w2)
    w_buf = win_buf.shape[1]
    w_all = jnp.concatenate([win_buf, win_kv], axis=1)
    w_pos = past - w_buf + jnp.arange(w_buf + Tn)
    o = nsa_branches(q, pos, gates, kc, vc, ks_blk, vs_blk, w_all[:, :, 0], w_all[:, :, 1], w_pos)
    return o.astype(h.dtype) @ w_out, cmp_kv, sel_kv, w_all[:, Tn:]


def diff_project(h, w_in, pos):
    B, T, _ = h.shape
    z = h @ w_in
    q = rope(z[..., :D_MODEL].reshape(B, T, DIFF_HEADS, 2, DIFF_DIM), pos)
    k = rope(z[..., D_MODEL:2 * D_MODEL].reshape(B, T, DIFF_HEADS, 2, DIFF_DIM), pos)
    v = z[..., 2 * D_MODEL:].reshape(B, T, DIFF_HEADS, 2 * DIFF_DIM)
    kv = jnp.stack([k.reshape(B, T, DIFF_HEADS, 2 * DIFF_DIM), v], axis=2)
    return q, kv


def diff_lambda_value(lam, lam_init):
    lf = lam.astype(jnp.float32)
    return jnp.exp(jnp.sum(lf[0] * lf[1])) - jnp.exp(jnp.sum(lf[2] * lf[3])) + lam_init


def diff_attend(q, q_pos, kv, k_pos, lam):
    f32 = jnp.float32
    B, Tk = kv.shape[:2]
    k = kv[:, :, 0].reshape(B, Tk, DIFF_HEADS, 2, DIFF_DIM).astype(f32)
    v = kv[:, :, 1].astype(f32)
    s = jnp.einsum('bthcd,bshcd->bhcts', q.astype(f32) * (DIFF_DIM ** -0.5), k)
    s = jnp.where(k_pos[None, :] <= q_pos[:, None], s, NEG)
    a = jax.nn.softmax(s, axis=-1)
    w = a[:, :, 0] - lam * a[:, :, 1]
    return jnp.einsum('bhts,bshe->bthe', w, v)


def diff_out(o, subnorm, lam_init, w_out, dtype):
    B, T = o.shape[:2]
    o = rms_norm(o, subnorm) * (1.0 - lam_init)
    return o.reshape(B, T, -1).astype(dtype) @ w_out


def diff_prompt(h, w_in, lam_p, subnorm, w_out, lam_init):
    B, T, _ = h.shape
    pos = jnp.arange(T)
    q, kv = diff_project(h, w_in, pos)
    lam = diff_lambda_value(lam_p, lam_init)

    def block(i):
        q0 = i * Q_BLOCK
        qb = lax.dynamic_slice_in_dim(q, q0, Q_BLOCK, axis=1)
        return diff_attend(qb, q0 + jnp.arange(Q_BLOCK), kv, pos, lam)

    o = lax.map(block, jnp.arange(T // Q_BLOCK))
    o = o.transpose(1, 0, 2, 3, 4).reshape(B, T, DIFF_HEADS, 2 * DIFF_DIM)
    return diff_out(o, subnorm, lam_init, w_out, h.dtype), kv


def diff_sample(h, cache_kv, page_table, w_in, lam_p, subnorm, w_out, lam_init):
    B, Tn, _ = h.shape
    past = page_table.shape[1] * cache_kv.shape[1]
    pos = past + jnp.arange(Tn)
    q, kv = diff_project(h, w_in, pos)
    lam = diff_lambda_value(lam_p, lam_init)
    kv_full = jnp.concatenate([gather_pages(cache_kv, page_table), kv], axis=1)
    o = diff_attend(q, pos, kv_full, jnp.arange(past + Tn), lam)
    return diff_out(o, subnorm, lam_init, w_out, h.dtype), kv


def setup_inputs(seed: int = 0) -> dict:
    key = jax.random.key(seed)
    ks = jax.random.split(key, 24)
    nrm = jax.random.normal
    f32 = jnp.float32
    n_pages = PAST_LEN // PAGE_SIZE
    n_used = DEC_BATCH * n_pages
    n_pool = n_used + n_used // 4
    perm = jax.random.permutation(ks[0], n_pool)
    page_table = perm[:n_used].reshape(DEC_BATCH, n_pages).astype(jnp.int32)
    Hk, dh = NSA_KV_HEADS, NSA_HEAD_DIM
    return {
        "x_prompt": nrm(ks[1], (BATCH, SEQ, D_MODEL), f32),
        "x_sample": nrm(ks[2], (DEC_BATCH, DEC_SEQ, D_MODEL), f32),
        "cache_nsa_cmp": nrm(ks[3], (N_NSA_LAYERS, n_pool, PAGE_SIZE, 2, Hk, dh), f32),
        "cache_nsa_sel": nrm(ks[4], (N_NSA_LAYERS, n_pool, PAGE_SIZE, 2, Hk, dh), f32),
        "state_nsa_win": nrm(ks[5], (N_NSA_LAYERS, DEC_BATCH, min(WINDOW, PAST_LEN), 2, Hk, dh), f32),
        "cache_diff_kv": nrm(ks[6], (N_DIFF_LAYERS, n_pool, PAGE_SIZE, 2, DIFF_HEADS, 2 * DIFF_DIM), f32),
        "page_table": page_table,
        "norm_mix": 1.0 + 0.02 * nrm(ks[7], (DEPTH, D_MODEL), f32),
        "norm_mlp": 1.0 + 0.02 * nrm(ks[8], (DEPTH, D_MODEL), f32),
        "norm_final": 1.0 + 0.02 * nrm(ks[9], (D_MODEL,), f32),
        "nsa_w_in": nrm(ks[10], (N_NSA_LAYERS, D_MODEL, NSA_IN), f32) * D_MODEL ** -0.5,
        "nsa_cmp_pe": 0.1 * nrm(ks[11], (N_NSA_LAYERS, 2, CMP_BLOCK, dh), f32),
        "nsa_cmp_w1": nrm(ks[12], (N_NSA_LAYERS, 2, CMP_BLOCK * dh, CMP_HIDDEN), f32) * (CMP_BLOCK * dh) ** -0.5,
        "nsa_cmp_w2": nrm(ks[13], (N_NSA_LAYERS, 2, CMP_HIDDEN, dh), f32) * CMP_HIDDEN ** -0.5,
        "nsa_w_out": nrm(ks[14], (N_NSA_LAYERS, NSA_HEADS * dh, D_MODEL), f32) * (NSA_HEADS * dh) ** -0.5,
        "diff_w_in": nrm(ks[15], (N_DIFF_LAYERS, D_MODEL, DIFF_IN), f32) * D_MODEL ** -0.5,
        "diff_lambda": 0.1 * nrm(ks[16], (N_DIFF_LAYERS, 4, DIFF_DIM), f32),
        "diff_subnorm": 1.0 + 0.02 * nrm(ks[17], (N_DIFF_LAYERS, 2 * DIFF_DIM), f32),
        "diff_w_out": nrm(ks[18], (N_DIFF_LAYERS, D_MODEL, D_MODEL), f32) * D_MODEL ** -0.5,
        "mlp_w_up": nrm(ks[19], (DEPTH, D_MODEL, D_FF), f32) * D_MODEL ** -0.5,
        "mlp_w_down": nrm(ks[20], (DEPTH, D_FF, D_MODEL), f32) * D_FF ** -0.5,
    }


def reference(x_prompt, x_sample, cache_nsa_cmp, cache_nsa_sel, state_nsa_win, cache_diff_kv, page_table,
              norm_mix, norm_mlp, norm_final, nsa_w_in, nsa_cmp_pe, nsa_cmp_w1, nsa_cmp_w2, nsa_w_out,
              diff_w_in, diff_lambda, diff_subnorm, diff_w_out, mlp_w_up, mlp_w_down):
    hp, hs = x_prompt, x_sample
    cmp_p, cmp_s, sel_p, sel_s, win_p, win_s, dkv_p, dkv_s = [], [], [], [], [], [], [], []
    for i in range(DEPTH):
        li = i // N_MIXERS
        ap = rms_norm(hp, norm_mix[i])
        a_s = rms_norm(hs, norm_mix[i])
        if i % N_MIXERS == 0:
            mp, c_p, s_p, w_p = nsa_prompt(ap, nsa_w_in[li], nsa_cmp_pe[li], nsa_cmp_w1[li], nsa_cmp_w2[li], nsa_w_out[li])
            ms, c_s, s_s, w_s = nsa_sample(a_s, cache_nsa_cmp[li], cache_nsa_sel[li], state_nsa_win[li], page_table,
                                           nsa_w_in[li], nsa_cmp_pe[li], nsa_cmp_w1[li], nsa_cmp_w2[li], nsa_w_out[li])
            cmp_p.append(c_p); cmp_s.append(c_s); sel_p.append(s_p); sel_s.append(s_s)
            win_p.append(w_p); win_s.append(w_s)
        else:
            lam_init = 0.8 - 0.6 * math.exp(-0.3 * i)
            mp, kv_p = diff_prompt(ap, diff_w_in[li], diff_lambda[li], diff_subnorm[li], diff_w_out[li], lam_init)
            ms, kv_s = diff_sample(a_s, cache_diff_kv[li], page_table, diff_w_in[li], diff_lambda[li],
                                   diff_subnorm[li], diff_w_out[li], lam_init)
            dkv_p.append(kv_p); dkv_s.append(kv_s)
        hp = hp + mp
        hs = hs + ms
        hp = hp + sq_relu_mlp(rms_norm(hp, norm_mlp[i]), mlp_w_up[i], mlp_w_down[i])
        hs = hs + sq_relu_mlp(rms_norm(hs, norm_mlp[i]), mlp_w_up[i], mlp_w_down[i])
    y_prompt = rms_norm(hp, norm_final)
    y_sample = rms_norm(hs, norm_final)
    new_nsa_cmp_prompt = jnp.stack(cmp_p)
    new_nsa_cmp_sample = jnp.stack(cmp_s)
    new_nsa_sel_prompt = jnp.stack(sel_p)
    new_nsa_sel_sample = jnp.stack(sel_s)
    new_nsa_win_prompt = jnp.stack(win_p)
    new_nsa_win_sample = jnp.stack(win_s)
    new_diff_kv_prompt = jnp.stack(dkv_p)
    new_diff_kv_sample = jnp.stack(dkv_s)
    return (y_prompt, y_sample, new_nsa_cmp_prompt, new_nsa_cmp_sample, new_nsa_sel_prompt, new_nsa_sel_sample,
            new_nsa_win_prompt, new_nsa_win_sample, new_diff_kv_prompt, new_diff_kv_sample)
```

```python
import functools
import math

import jax
import jax.numpy as jnp
from jax import lax
from jax.experimental import pallas as pl
from jax.experimental.pallas import tpu as pltpu

F32 = jnp.float32
BF16 = jnp.bfloat16

D_MODEL = 1024
NSA_HEADS = 16
NSA_KV_HEADS = 2
HEAD_DIM = 64
NSA_GROUP = NSA_HEADS // NSA_KV_HEADS
CMP_BLOCK = 32
CMP_STRIDE = 16
SEL_BLOCK = 64
SEL_TOPK = 16
WINDOW = 512
DIFF_HEADS = 8
DIFF_DIM = 64
ROPE_THETA = 10000.0
EPS = 1e-6
NEG = -1e30
FORCE = 1e6
REMOVED = -3e38
LANES = 128
Q_SCALE = HEAD_DIM ** -0.5

VMEM_LIMIT = 56 * 1024 * 1024


def _params(n_axes, vmem=None):
    return pltpu.CompilerParams(dimension_semantics=("arbitrary",) * n_axes,
                                vmem_limit_bytes=vmem or VMEM_LIMIT)


def _rope_tables(pos):
    half = HEAD_DIM // 2
    inv = ROPE_THETA ** (-jnp.arange(half, dtype=F32) / half)
    ang = pos.astype(F32)[:, None] * inv
    cos, sin = jnp.cos(ang), jnp.sin(ang)
    return jnp.tile(cos, (1, 4)), jnp.tile(jnp.concatenate([-sin, sin], axis=1), (1, 2))


def _rope128(x, cos, sin_signed):
    lane = lax.broadcasted_iota(jnp.int32, x.shape, 1)
    first = (lane & (HEAD_DIM - 1)) < HEAD_DIM // 2
    rot = jnp.where(first, pltpu.roll(x, LANES - HEAD_DIM // 2, 1), pltpu.roll(x, HEAD_DIM // 2, 1))
    return x * cos + rot * sin_signed


def _rmsnorm(x, g):
    return x * lax.rsqrt(jnp.mean(x * x, axis=-1, keepdims=True) + EPS) * g


def _dot(a, b):
    return jnp.dot(a, b, preferred_element_type=F32)


def _dot_nt(a, b):
    return lax.dot_general(a, b, (((1,), (1,)), ((), ())), preferred_element_type=F32)


def _pick(cond, a, b):
    if isinstance(cond, bool):
        return a if cond else b
    return jnp.where(cond, a, b)


def _stack_q(qblk, hk):
    tq = qblk.shape[0]
    lane = lax.broadcasted_iota(jnp.int32, (tq, LANES), 1)
    keep = (lane >> 6) == hk
    parts = []
    for g in range(NSA_GROUP):
        qp = qblk[:, (g // 2) * LANES:(g // 2 + 1) * LANES]
        qp = _pick(hk == (g % 2), qp, pltpu.roll(qp, HEAD_DIM, 1))
        parts.append(jnp.where(keep, qp, 0.0))
    return jnp.concatenate(parts, axis=0).astype(BF16)


def _unstack_o(o2, hk, tq):
    lane = lax.broadcasted_iota(jnp.int32, (tq, LANES), 1)
    outs = []
    for jp in range(NSA_GROUP // 2):
        ev = o2[(2 * jp) * tq:(2 * jp + 1) * tq]
        od = o2[(2 * jp + 1) * tq:(2 * jp + 2) * tq]
        ev = _pick(hk == 0, ev, pltpu.roll(ev, HEAD_DIM, 1))
        od = _pick(hk == 1, od, pltpu.roll(od, HEAD_DIM, 1))
        outs.append(jnp.where(lane < HEAD_DIM, ev, od))
    return outs


def _split_bf16(x):
    hi = x.astype(BF16)
    lo = (x - hi.astype(F32)).astype(BF16)
    return hi, lo


def _mask_rows(x, ok, fill):
    if ok is None:
        return x
    if ok.shape == x.shape:
        return jnp.where(ok, x, fill)
    g = x.shape[0] // ok.shape[0]
    x3 = x.reshape(g, ok.shape[0], x.shape[1])
    return jnp.where(ok[None], x3, fill).reshape(x.shape)


def _softmax_step(s, ok, m_ref, l_ref, acc_ref, v):
    s = _mask_rows(s, ok, NEG)
    m_prev = m_ref[...]
    m_new = jnp.maximum(m_prev, jnp.max(s, axis=-1, keepdims=True))
    alpha = jnp.exp(m_prev - m_new)
    p = _mask_rows(jnp.exp(s - m_new), ok, 0.0)
    l_ref[...] = alpha * l_ref[...] + jnp.sum(p, axis=-1, keepdims=True)
    acc_ref[...] = alpha * acc_ref[...] + _dot(p.astype(BF16), v)
    m_ref[...] = m_new


def _softmax_once(s, ok, v):
    s = _mask_rows(s, ok, NEG)
    p = _mask_rows(jnp.exp(s - jnp.max(s, axis=-1, keepdims=True)), ok, 0.0)
    return _dot(p.astype(BF16), v) / jnp.sum(p, axis=-1, keepdims=True)


def _tile_rows(x, n):
    return jnp.concatenate([x] * n, axis=0)


def _nsa_proj_kernel(x_ref, g_ref, wq_ref, wkv_ref, wg_ref, cos_ref, sin_ref,
                     q_ref, kv_ref, kvb_ref, gate_ref):
    xn = _rmsnorm(x_ref[...], g_ref[...]).astype(BF16)
    cos = cos_ref[...]
    sin = sin_ref[...]
    zq = _dot(xn, wq_ref[...])
    for j in range(zq.shape[1] // LANES):
        sl = slice(j * LANES, (j + 1) * LANES)
        q_ref[:, sl] = _rope128(zq[:, sl], cos, sin) * Q_SCALE
    zkv = _dot(xn, wkv_ref[...])
    for j in range(zkv.shape[1] // LANES):
        sl = slice(j * LANES, (j + 1) * LANES)
        blk = zkv[:, sl]
        if j in (2, 4):
            blk = _rope128(blk, cos, sin)
        kv_ref[:, sl] = blk
        kvb_ref[:, sl] = blk.astype(BF16)
    gate_ref[...] = jax.nn.sigmoid(_dot(xn, wg_ref[...]))


def _nsa_proj(x, g, wq, wkv, wg, cos, sin, tm):
    n = x.shape[0]
    row = lambda i: (i, 0)
    full = lambda i: (0, 0)
    return pl.pallas_call(
        _nsa_proj_kernel,
        grid=(n // tm,),
        in_specs=[pl.BlockSpec((tm, D_MODEL), row), pl.BlockSpec((1, D_MODEL), full),
                  pl.BlockSpec(wq.shape, full), pl.BlockSpec(wkv.shape, full), pl.BlockSpec(wg.shape, full),
                  pl.BlockSpec((tm, LANES), row), pl.BlockSpec((tm, LANES), row)],
        out_specs=[pl.BlockSpec((tm, wq.shape[1]), row), pl.BlockSpec((tm, wkv.shape[1]), row),
                   pl.BlockSpec((tm, wkv.shape[1]), row), pl.BlockSpec((tm, LANES), row)],
        out_shape=[jax.ShapeDtypeStruct((n, wq.shape[1]), F32), jax.ShapeDtypeStruct((n, wkv.shape[1]), F32),
                   jax.ShapeDtypeStruct((n, wkv.shape[1]), BF16), jax.ShapeDtypeStruct((n, LANES), F32)],
        compiler_params=_params(1), name="nsa_proj",
    )(x, g, wq, wkv, wg, cos, sin)


def _compress_kernel(x_ref, w1_ref, w2_ref, pe_ref, cos_ref, sin_ref, o_ref):
    nc = x_ref.shape[2]
    half = CMP_STRIDE * HEAD_DIM
    w1 = w1_ref[0]
    const = _dot(pe_ref[0].astype(BF16), w1)[0:1]
    out = jnp.zeros((nc, LANES), F32)
    for hk in range(NSA_KV_HEADS):
        x = x_ref[0, hk].astype(BF16)
        a = _dot(x, w1[:half])
        b = _dot(x, w1[half:])
        hid = a + pltpu.roll(b, nc - 1, 0) + const
        hid = hid * jax.nn.sigmoid(hid)
        out = out + _dot(hid.astype(BF16), w2_ref[0, hk])
    rowid = lax.broadcasted_iota(jnp.int32, out.shape, 0)
    out = jnp.where(rowid < nc - 1, out, 0.0)
    o_ref[0, 0] = _rope128(out, cos_ref[0], sin_ref[0])


def _compress(xc, w1, w2w, pe, cos, sin):
    s, _, nc, width = xc.shape
    return pl.pallas_call(
        _compress_kernel,
        grid=(s, 2),
        in_specs=[pl.BlockSpec((1, 2, nc, width), lambda i, c: (i, c, 0, 0)),
                  pl.BlockSpec((1,) + w1.shape[1:], lambda i, c: (c, 0, 0)),
                  pl.BlockSpec((1,) + w2w.shape[1:], lambda i, c: (c, 0, 0, 0)),
                  pl.BlockSpec((1,) + pe.shape[1:], lambda i, c: (c, 0, 0)),
                  pl.BlockSpec((1, nc, LANES), lambda i, c: (c, 0, 0)),
                  pl.BlockSpec((1, nc, LANES), lambda i, c: (c, 0, 0))],
        out_specs=pl.BlockSpec((1, 1, nc, LANES), lambda i, c: (i, c, 0, 0)),
        out_shape=jax.ShapeDtypeStruct((s, 2, nc, LANES), F32),
        compiler_params=_params(2), name="nsa_compress",
    )(xc, w1, w2w, pe, cos, sin)


def _cmp_select_kernel(q_ref, kvc_ref, ocmp_ref, sel_ref, *, tq, pos0_fn, n_blocks, n_pick):
    nc = kvc_ref.shape[2]
    pos0 = pos0_fn()
    kc = kvc_ref[0, 0].astype(BF16)
    vc = kvc_ref[0, 1].astype(BF16)
    t_io = lax.broadcasted_iota(jnp.int32, (tq, nc), 0)
    c_io = lax.broadcasted_iota(jnp.int32, (tq, nc), 1)
    ok = ((c_io * CMP_STRIDE + CMP_BLOCK - 1) <= (pos0 + t_io)) & (c_io < nc - 1)
    ci = lax.broadcasted_iota(jnp.int32, (nc, LANES), 0)
    sj = lax.broadcasted_iota(jnp.int32, (nc, LANES), 1)
    overlap = ((ci * CMP_STRIDE < (sj + 1) * SEL_BLOCK) & (ci * CMP_STRIDE + CMP_BLOCK > sj * SEL_BLOCK)
               & (ci < nc - 1)).astype(BF16)
    j_io = lax.broadcasted_iota(jnp.int32, (tq, LANES), 1)
    qpos = pos0 + lax.broadcasted_iota(jnp.int32, (tq, LANES), 0)
    cur = qpos >> 6
    forced = (j_io == 0) | (j_io == cur) | (j_io == cur - 1)
    causal = (j_io * SEL_BLOCK) <= qpos
    jf = j_io.astype(F32)
    for hk in range(NSA_KV_HEADS):
        qs = _stack_q(q_ref[:, hk * 512:(hk + 1) * 512], hk)
        s = _mask_rows(_dot_nt(qs, kc), ok, NEG)
        m = jnp.max(s, axis=-1, keepdims=True)
        p = _mask_rows(jnp.exp(s - m), ok, 0.0)
        l = jnp.sum(p, axis=-1, keepdims=True)
        p = p / jnp.where(l > 0.0, l, 1.0)
        o2 = _dot(p.astype(BF16), vc)
        for jp, blk in enumerate(_unstack_o(o2, hk, tq)):
            c0 = (hk * 4 + jp) * LANES
            ocmp_ref[:, c0:c0 + LANES] = blk
        psum = p[0:tq]
        for g in range(1, NSA_GROUP):
            psum = psum + p[g * tq:(g + 1) * tq]
        hi, lo = _split_bf16(psum)
        imp = _dot(hi, overlap) + _dot(lo, overlap)
        x = jnp.where(forced, FORCE, jnp.where(causal, imp, -1.0))
        x = jnp.where(j_io < n_blocks, x, REMOVED)
        sel = jnp.zeros((tq, LANES), F32)
        for _ in range(n_pick):
            mx = jnp.max(x, axis=-1, keepdims=True)
            idx = jnp.min(jnp.where(x == mx, jf, float(LANES)), axis=-1, keepdims=True)
            hit = jf == idx
            sel = jnp.where(hit, 1.0, sel)
            x = jnp.where(hit, REMOVED, x)
        sel_ref[:, hk * LANES:(hk + 1) * LANES] = sel


def _cmp_select(q, kvc, *, tq, blocks_per_seq, pos0_of_block, n_blocks, n_pick):
    n = q.shape[0]
    nc = kvc.shape[2]
    kern = functools.partial(_cmp_select_kernel, tq=tq, n_blocks=n_blocks, n_pick=n_pick,
                             pos0_fn=lambda: pos0_of_block(pl.program_id(1)))
    row = lambda s, i: (s * blocks_per_seq + i, 0)
    return pl.pallas_call(
        kern,
        grid=(n // (tq * blocks_per_seq), blocks_per_seq),
        in_specs=[pl.BlockSpec((tq, D_MODEL), row),
                  pl.BlockSpec((1, 2, nc, LANES), lambda s, i: (s, 0, 0, 0))],
        out_specs=[pl.BlockSpec((tq, D_MODEL), row), pl.BlockSpec((tq, 2 * LANES), row)],
        out_shape=[jax.ShapeDtypeStruct((n, D_MODEL), F32), jax.ShapeDtypeStruct((n, 2 * LANES), F32)],
        compiler_params=_params(2), name="nsa_cmp_select",
    )(q, kvc)


def _expand_sel(selm, key0, tk):
    jb = lax.broadcasted_iota(jnp.int32, (LANES, tk), 0)
    kk = lax.broadcasted_iota(jnp.int32, (LANES, tk), 1)
    expand = (((key0 + kk) >> 6) == jb).astype(BF16)
    return _dot(selm.astype(BF16), expand) > 0.5


def _sel_prompt_kernel(qi_ref, kj_ref, q_ref, k_ref, v_ref, sel_ref, o_ref, qs_ref, m_ref, l_ref, acc_ref,
                       *, tq, tk):
    step = pl.program_id(1)
    hk = pl.program_id(0) % 2
    qi = qi_ref[step]
    kj = kj_ref[step]

    @pl.when(kj == 0)
    def _():
        qs_ref[...] = _stack_q(q_ref[...], hk)
        m_ref[...] = jnp.full(m_ref.shape, NEG, F32)
        l_ref[...] = jnp.zeros(l_ref.shape, F32)
        acc_ref[...] = jnp.zeros(acc_ref.shape, F32)

    s = _dot_nt(qs_ref[...], k_ref[...])
    qpos = qi * tq + lax.broadcasted_iota(jnp.int32, (tq, tk), 0)
    kpos = kj * tk + lax.broadcasted_iota(jnp.int32, (tq, tk), 1)
    ok = _expand_sel(sel_ref[...], kj * tk, tk) & (kpos <= qpos)
    _softmax_step(s, ok, m_ref, l_ref, acc_ref, v_ref[...])

    @pl.when(kj == ((qi + 1) * tq - 1) // tk)
    def _():
        o2 = acc_ref[...] / l_ref[...]
        for jp, blk in enumerate(_unstack_o(o2, hk, tq)):
            o_ref[:, jp * LANES:(jp + 1) * LANES] = blk


def _causal_steps(n_q, tq, tk):
    qi, kj = [], []
    for i in range(n_q):
        for j in range(((i + 1) * tq - 1) // tk + 1):
            qi.append(i)
            kj.append(j)
    return jnp.asarray(qi, jnp.int32), jnp.asarray(kj, jnp.int32)


def _sel_prompt(q, kvb, selmask, *, n_seq, t_len, tq, tk, k_col, v_col):
    n = q.shape[0]
    nqb = t_len // tq
    nkb = t_len // tk
    qi, kj = _causal_steps(nqb, tq, tk)
    rows = NSA_GROUP * tq
    grid_spec = pltpu.PrefetchScalarGridSpec(
        num_scalar_prefetch=2,
        grid=(n_seq * 2, qi.shape[0]),
        in_specs=[pl.BlockSpec((tq, 512), lambda a, s, qi, kj: ((a // 2) * nqb + qi[s], a % 2)),
                  pl.BlockSpec((tk, LANES), lambda a, s, qi, kj: ((a // 2) * nkb + kj[s], k_col)),
                  pl.BlockSpec((tk, LANES), lambda a, s, qi, kj: ((a // 2) * nkb + kj[s], v_col)),
                  pl.BlockSpec((tq, LANES), lambda a, s, qi, kj: ((a // 2) * nqb + qi[s], a % 2))],
        out_specs=pl.BlockSpec((tq, 512), lambda a, s, qi, kj: ((a // 2) * nqb + qi[s], a % 2)),
        scratch_shapes=[pltpu.VMEM((rows, LANES), BF16), pltpu.VMEM((rows, 1), F32),
                        pltpu.VMEM((rows, 1), F32), pltpu.VMEM((rows, LANES), F32)],
    )
    return pl.pallas_call(
        functools.partial(_sel_prompt_kernel, tq=tq, tk=tk),
        grid_spec=grid_spec,
        out_shape=jax.ShapeDtypeStruct((n, D_MODEL), F32),
        compiler_params=_params(2), name="nsa_sel_prompt",
    )(qi, kj, q, kvb, kvb, selmask)


def _win_prompt_kernel(q_ref, *refs, tq, n_tiles):
    k_refs = refs[:n_tiles]
    v_refs = refs[n_tiles:2 * n_tiles]
    o_ref = refs[2 * n_tiles]
    i = pl.program_id(1)
    k = jnp.concatenate([r[...] for r in k_refs], axis=0)
    v = jnp.concatenate([r[...] for r in v_refs], axis=0)
    nk = n_tiles * tq
    qpos = i * tq + lax.broadcasted_iota(jnp.int32, (tq, nk), 0)
    kpos = (i - (n_tiles - 1)) * tq + lax.broadcasted_iota(jnp.int32, (tq, nk), 1)
    ok = (kpos <= qpos) & (kpos > qpos - WINDOW) & (kpos >= 0)
    for hk in range(NSA_KV_HEADS):
        qs = _stack_q(q_ref[:, hk * 512:(hk + 1) * 512], hk)
        o2 = _softmax_once(_dot_nt(qs, k), ok, v)
        for jp, blk in enumerate(_unstack_o(o2, hk, tq)):
            c0 = (hk * 4 + jp) * LANES
            o_ref[:, c0:c0 + LANES] = blk


def _win_prompt(q, kvb, *, n_seq, t_len, tq, k_col, v_col):
    n = q.shape[0]
    nqb = t_len // tq
    n_tiles = WINDOW // tq + 1

    def kv_spec(d, col):
        return pl.BlockSpec((tq, LANES), lambda s, i: (jnp.maximum(s * nqb + i - (n_tiles - 1) + d, 0), col))

    row = lambda s, i: (s * nqb + i, 0)
    return pl.pallas_call(
        functools.partial(_win_prompt_kernel, tq=tq, n_tiles=n_tiles),
        grid=(n_seq, nqb),
        in_specs=[pl.BlockSpec((tq, D_MODEL), row)] + [kv_spec(d, k_col) for d in range(n_tiles)]
                 + [kv_spec(d, v_col) for d in range(n_tiles)],
        out_specs=pl.BlockSpec((tq, D_MODEL), row),
        out_shape=jax.ShapeDtypeStruct((n, D_MODEL), F32),
        compiler_params=_params(2), name="nsa_win_prompt",
    )(q, *([kvb] * (2 * n_tiles)))


def _stack_q_both(q_ref):
    return jnp.concatenate([_stack_q(q_ref[:, hk * 512:(hk + 1) * 512], hk)
                            for hk in range(NSA_KV_HEADS)], axis=0)


def _expand_sel_both(sel_ref, key0, tk):
    rows = jnp.concatenate([_tile_rows(sel_ref[:, hk * LANES:(hk + 1) * LANES], NSA_GROUP)
                            for hk in range(NSA_KV_HEADS)], axis=0)
    return _expand_sel(rows, key0, tk)


def _pad_rows(x, n):
    return jnp.concatenate([x, jnp.zeros((n - x.shape[0], x.shape[1]), x.dtype)], axis=0)


def _write_unstacked_both(o2, o_ref, tn):
    rows = NSA_GROUP * tn
    for hk in range(NSA_KV_HEADS):
        for jp, blk in enumerate(_unstack_o(o2[hk * rows:(hk + 1) * rows], hk, tn)):
            c0 = (hk * 4 + jp) * LANES
            o_ref[:, c0:c0 + LANES] = blk


def _sel_sample_kernel(pt_ref, q_ref, page_ref, sel_ref, kn_ref, vn_ref, o_ref, qs_ref, m_ref, l_ref, acc_ref,
                       *, tn, page):
    p_id = pl.program_id(1)
    rows = NSA_GROUP * tn

    @pl.when(p_id == 0)
    def _():
        qs_ref[...] = _stack_q_both(q_ref)
        m_ref[...] = jnp.full(m_ref.shape, NEG, F32)
        l_ref[...] = jnp.zeros(l_ref.shape, F32)
        acc_ref[...] = jnp.zeros(acc_ref.shape, F32)

    pg = page_ref[0]
    k = pg[:, 0:LANES].astype(BF16)
    v = pg[:, LANES:2 * LANES].astype(BF16)
    s = _dot_nt(qs_ref[...], k)
    _softmax_step(s, _expand_sel_both(sel_ref, p_id * page, page), m_ref, l_ref, acc_ref, v)

    @pl.when(p_id == pl.num_programs(1) - 1)
    def _():
        kn = _pad_rows(kn_ref[...], LANES).astype(BF16)
        vn = _pad_rows(vn_ref[...], LANES).astype(BF16)
        sn = _dot_nt(qs_ref[...], kn)
        r_io = lax.broadcasted_iota(jnp.int32, (2 * rows, LANES), 0)
        c_io = lax.broadcasted_iota(jnp.int32, (2 * rows, LANES), 1)
        okn = (c_io < tn) & (c_io <= (r_io & (tn - 1)))
        _softmax_step(sn, okn, m_ref, l_ref, acc_ref, vn)
        _write_unstacked_both(acc_ref[...] / l_ref[...], o_ref, tn)


def _sel_sample(q, cache, page_table, selmask, kv_new, *, tn, k_col, v_col):
    n = q.shape[0]
    n_seq, n_pages = page_table.shape
    page = cache.shape[1]
    rows = 2 * NSA_GROUP * tn
    grid_spec = pltpu.PrefetchScalarGridSpec(
        num_scalar_prefetch=1,
        grid=(n_seq, n_pages),
        in_specs=[pl.BlockSpec((tn, D_MODEL), lambda s, p, pt: (s, 0)),
                  pl.BlockSpec((1, page, cache.shape[2]), lambda s, p, pt: (pt[s * n_pages + p], 0, 0)),
                  pl.BlockSpec((tn, 2 * LANES), lambda s, p, pt: (s, 0)),
                  pl.BlockSpec((tn, LANES), lambda s, p, pt: (s, k_col)),
                  pl.BlockSpec((tn, LANES), lambda s, p, pt: (s, v_col))],
        out_specs=pl.BlockSpec((tn, D_MODEL), lambda s, p, pt: (s, 0)),
        scratch_shapes=[pltpu.VMEM((rows, LANES), BF16), pltpu.VMEM((rows, 1), F32),
                        pltpu.VMEM((rows, 1), F32), pltpu.VMEM((rows, LANES), F32)],
    )
    return pl.pallas_call(
        functools.partial(_sel_sample_kernel, tn=tn, page=page),
        grid_spec=grid_spec,
        out_shape=jax.ShapeDtypeStruct((n, D_MODEL), F32),
        compiler_params=_params(2), name="nsa_sel_sample",
    )(page_table.reshape(-1), q, cache, selmask, kv_new, kv_new)


def _win_sample_kernel(q_ref, buf_ref, kn_ref, vn_ref, o_ref, *, tn, past):
    w_buf = buf_ref.shape[1]
    rows = 2 * NSA_GROUP * tn
    qs = _stack_q_both(q_ref)
    buf = buf_ref[0]
    k = jnp.concatenate([buf[:, 0:LANES], _pad_rows(kn_ref[...], LANES)], axis=0).astype(BF16)
    v = jnp.concatenate([buf[:, LANES:2 * LANES], _pad_rows(vn_ref[...], LANES)], axis=0).astype(BF16)
    nk = w_buf + LANES
    r_io = lax.broadcasted_iota(jnp.int32, (rows, nk), 0)
    c_io = lax.broadcasted_iota(jnp.int32, (rows, nk), 1)
    qpos = past + (r_io & (tn - 1))
    kpos = past - w_buf + c_io
    ok = (kpos <= qpos) & (kpos > qpos - WINDOW) & (kpos >= 0) & (c_io < w_buf + tn)
    _write_unstacked_both(_softmax_once(_dot_nt(qs, k), ok, v), o_ref, tn)


def _win_sample(q, win_buf, kv_new, *, tn, past, k_col, v_col):
    n = q.shape[0]
    n_seq, w_buf, width = win_buf.shape
    return pl.pallas_call(
        functools.partial(_win_sample_kernel, tn=tn, past=past),
        grid=(n_seq,),
        in_specs=[pl.BlockSpec((tn, D_MODEL), lambda s: (s, 0)),
                  pl.BlockSpec((1, w_buf, width), lambda s: (s, 0, 0)),
                  pl.BlockSpec((tn, LANES), lambda s: (s, k_col)),
                  pl.BlockSpec((tn, LANES), lambda s: (s, v_col))],
        out_specs=pl.BlockSpec((tn, D_MODEL), lambda s: (s, 0)),
        out_shape=jax.ShapeDtypeStruct((n, D_MODEL), F32),
        compiler_params=_params(1), name="nsa_win_sample",
    )(q, win_buf, kv_new, kv_new)


def _nsa_out_kernel(x_ref, oc_ref, os_ref, ow_ref, gate_ref, e_ref, w_ref, y_ref):
    hi, lo = _split_bf16(gate_ref[...])
    comb = jnp.zeros(oc_ref.shape, F32)
    for br, o_ref in enumerate((oc_ref, os_ref, ow_ref)):
        e = e_ref[br]
        comb = comb + (_dot(hi, e) + _dot(lo, e)) * o_ref[...]
    y_ref[...] = x_ref[...] + _dot(comb.astype(BF16), w_ref[...])


def _nsa_out(x, oc, osel, ow, gates, expand, w, tm):
    n = x.shape[0]
    row = lambda i: (i, 0)
    return pl.pallas_call(
        _nsa_out_kernel,
        grid=(n // tm,),
        in_specs=[pl.BlockSpec((tm, D_MODEL), row)] * 4
                 + [pl.BlockSpec((tm, LANES), row), pl.BlockSpec(expand.shape, lambda i: (0, 0, 0)),
                    pl.BlockSpec(w.shape, lambda i: (0, 0))],
        out_specs=pl.BlockSpec((tm, D_MODEL), row),
        out_shape=jax.ShapeDtypeStruct((n, D_MODEL), F32),
        compiler_params=_params(1), name="nsa_out",
    )(x, oc, osel, ow, gates, expand, w)


def _res_proj_kernel(x_ref, o_ref, w_ref, y_ref):
    y_ref[...] = x_ref[...] + _dot(o_ref[...].astype(BF16), w_ref[...])


def _res_proj(x, o, w, tm):
    n = x.shape[0]
    row = lambda i: (i, 0)
    return pl.pallas_call(
        _res_proj_kernel,
        grid=(n // tm,),
        in_specs=[pl.BlockSpec((tm, D_MODEL), row), pl.BlockSpec((tm, D_MODEL), row),
                  pl.BlockSpec(w.shape, lambda i: (0, 0))],
        out_specs=pl.BlockSpec((tm, D_MODEL), row),
        out_shape=jax.ShapeDtypeStruct((n, D_MODEL), F32),
        compiler_params=_params(1), name="res_proj",
    )(x, o, w)


def _mlp_kernel(x_ref, g_ref, wu_ref, wd_ref, gf_ref, y_ref, *, tf, final_norm):
    x = x_ref[...]
    xn = _rmsnorm(x, g_ref[...]).astype(BF16)
    acc = x
    for c in range(wu_ref.shape[1] // tf):
        u = jnp.maximum(_dot(xn, wu_ref[:, c * tf:(c + 1) * tf]), 0.0)
        acc = acc + _dot((u * u).astype(BF16), wd_ref[c * tf:(c + 1) * tf, :])
    if final_norm:
        acc = _rmsnorm(acc, gf_ref[...])
    y_ref[...] = acc


def _mlp(x, g, wu, wd, gf, tm, final_norm):
    n = x.shape[0]
    row = lambda i: (i, 0)
    full = lambda i: (0, 0)
    return pl.pallas_call(
        functools.partial(_mlp_kernel, tf=512, final_norm=final_norm),
        grid=(n // tm,),
        in_specs=[pl.BlockSpec((tm, D_MODEL), row), pl.BlockSpec((1, D_MODEL), full),
                  pl.BlockSpec(wu.shape, full), pl.BlockSpec(wd.shape, full), pl.BlockSpec((1, D_MODEL), full)],
        out_specs=pl.BlockSpec((tm, D_MODEL), row),
        out_shape=jax.ShapeDtypeStruct((n, D_MODEL), F32),
        compiler_params=_params(1), name="mlp",
    )(x, g, wu, wd, gf)


def _diff_proj_kernel(x_ref, g_ref, w_ref, cos_ref, sin_ref, q_ref, kv_ref, kvb_ref):
    xn = _rmsnorm(x_ref[...], g_ref[...]).astype(BF16)
    cos = cos_ref[...]
    sin = sin_ref[...]
    for part in range(3):
        z = _dot(xn, w_ref[:, part * D_MODEL:(part + 1) * D_MODEL])
        for j in range(D_MODEL // LANES):
            sl = slice(j * LANES, (j + 1) * LANES)
            blk = z[:, sl]
            if part == 0:
                q_ref[:, sl] = _rope128(blk, cos, sin) * Q_SCALE
            else:
                if part == 1:
                    blk = _rope128(blk, cos, sin)
                so = slice((part - 1) * D_MODEL + j * LANES, (part - 1) * D_MODEL + (j + 1) * LANES)
                kv_ref[:, so] = blk
                kvb_ref[:, so] = blk.astype(BF16)


def _diff_proj(x, g, w, cos, sin, tm):
    n = x.shape[0]
    row = lambda i: (i, 0)
    full = lambda i: (0, 0)
    return pl.pallas_call(
        _diff_proj_kernel,
        grid=(n // tm,),
        in_specs=[pl.BlockSpec((tm, D_MODEL), row), pl.BlockSpec((1, D_MODEL), full), pl.BlockSpec(w.shape, full),
                  pl.BlockSpec((tm, LANES), row), pl.BlockSpec((tm, LANES), row)],
        out_specs=[pl.BlockSpec((tm, D_MODEL), row), pl.BlockSpec((tm, 2 * D_MODEL), row),
                   pl.BlockSpec((tm, 2 * D_MODEL), row)],
        out_shape=[jax.ShapeDtypeStruct((n, D_MODEL), F32), jax.ShapeDtypeStruct((n, 2 * D_MODEL), F32),
                   jax.ShapeDtypeStruct((n, 2 * D_MODEL), BF16)],
        compiler_params=_params(1), name="diff_proj",
    )(x, g, w, cos, sin)


def _lambda_value(lam_ref, lam_init):
    lf = lam_ref[...]
    return (jnp.exp(jnp.sum(lf[0:1] * lf[1:2], keepdims=True))
            - jnp.exp(jnp.sum(lf[2:3] * lf[3:4], keepdims=True)) + lam_init)


def _diff_finish(o0, o1, lam, sub, lam_init):
    o = o0 - lam * o1
    return _rmsnorm(o, sub) * (1.0 - lam_init)


def _diff_prompt_kernel(qi_ref, kj_ref, q_ref, k_ref, v_ref, lam_ref, sub_ref, o_ref,
                        qs_ref, m_ref, l_ref, acc_ref, *, tq, tk, lam_init):
    step = pl.program_id(1)
    qi = qi_ref[step]
    kj = kj_ref[step]

    @pl.when(kj == 0)
    def _():
        q = q_ref[...]
        lane = lax.broadcasted_iota(jnp.int32, q.shape, 1)
        qs_ref[...] = jnp.concatenate([jnp.where(lane < DIFF_DIM, q, 0.0),
                                       jnp.where(lane >= DIFF_DIM, q, 0.0)], axis=0).astype(BF16)
        m_ref[...] = jnp.full(m_ref.shape, NEG, F32)
        l_ref[...] = jnp.zeros(l_ref.shape, F32)
        acc_ref[...] = jnp.zeros(acc_ref.shape, F32)

    s = _dot_nt(qs_ref[...], k_ref[...])
    qpos = qi * tq + lax.broadcasted_iota(jnp.int32, (tq, tk), 0)
    kpos = kj * tk + lax.broadcasted_iota(jnp.int32, (tq, tk), 1)
    _softmax_step(s, kpos <= qpos, m_ref, l_ref, acc_ref, v_ref[...])

    @pl.when(kj == ((qi + 1) * tq - 1) // tk)
    def _():
        o = acc_ref[...] / l_ref[...]
        o_ref[...] = _diff_finish(o[0:tq], o[tq:2 * tq], _lambda_value(lam_ref, lam_init), sub_ref[...], lam_init)


def _diff_prompt(q, kvb, lam_p, sub, *, n_seq, t_len, tq, tk, lam_init):
    n = q.shape[0]
    nqb = t_len // tq
    nkb = t_len // tk
    qi, kj = _causal_steps(nqb, tq, tk)
    hd = DIFF_HEADS
    grid_spec = pltpu.PrefetchScalarGridSpec(
        num_scalar_prefetch=2,
        grid=(n_seq * hd, qi.shape[0]),
        in_specs=[pl.BlockSpec((tq, LANES), lambda a, s, qi, kj: ((a // hd) * nqb + qi[s], a % hd)),
                  pl.BlockSpec((tk, LANES), lambda a, s, qi, kj: ((a // hd) * nkb + kj[s], a % hd)),
                  pl.BlockSpec((tk, LANES), lambda a, s, qi, kj: ((a // hd) * nkb + kj[s], hd + a % hd)),
                  pl.BlockSpec(lam_p.shape, lambda a, s, qi, kj: (0, 0)),
                  pl.BlockSpec(sub.shape, lambda a, s, qi, kj: (0, 0))],
        out_specs=pl.BlockSpec((tq, LANES), lambda a, s, qi, kj: ((a // hd) * nqb + qi[s], a % hd)),
        scratch_shapes=[pltpu.VMEM((2 * tq, LANES), BF16), pltpu.VMEM((2 * tq, 1), F32),
                        pltpu.VMEM((2 * tq, 1), F32), pltpu.VMEM((2 * tq, LANES), F32)],
    )
    return pl.pallas_call(
        functools.partial(_diff_prompt_kernel, tq=tq, tk=tk, lam_init=lam_init),
        grid_spec=grid_spec,
        out_shape=jax.ShapeDtypeStruct((n, D_MODEL), F32),
        compiler_params=_params(2), name="diff_prompt",
    )(qi, kj, q, kvb, kvb, lam_p, sub)


def _diff_sample_kernel(pt_ref, q_ref, page_ref, kvn_ref, lam_ref, sub_ref, o_ref, qs_ref, m_ref, l_ref, acc_ref,
                        *, tn, lam_init):
    p_id = pl.program_id(1)
    rows = 2 * DIFF_HEADS * tn

    @pl.when(p_id == 0)
    def _():
        q = q_ref[...]
        lane = lax.broadcasted_iota(jnp.int32, q.shape, 1)
        parts = [jnp.where((lane >> 6) == hc, q, 0.0) for hc in range(2 * DIFF_HEADS)]
        qs_ref[...] = jnp.concatenate(parts, axis=0).astype(BF16)
        m_ref[...] = jnp.full(m_ref.shape, NEG, F32)
        l_ref[...] = jnp.zeros(l_ref.shape, F32)
        acc_ref[...] = jnp.zeros(acc_ref.shape, F32)

    pg = page_ref[0]
    s = _dot_nt(qs_ref[...], pg[:, 0:D_MODEL].astype(BF16))
    _softmax_step(s, None, m_ref, l_ref, acc_ref, pg[:, D_MODEL:2 * D_MODEL].astype(BF16))

    @pl.when(p_id == pl.num_programs(1) - 1)
    def _():
        kvn = _pad_rows(kvn_ref[...], LANES)
        sn = _dot_nt(qs_ref[...], kvn[:, 0:D_MODEL].astype(BF16))
        r_io = lax.broadcasted_iota(jnp.int32, (rows, LANES), 0)
        c_io = lax.broadcasted_iota(jnp.int32, (rows, LANES), 1)
        okn = (c_io < tn) & (c_io <= (r_io & (tn - 1)))
        _softmax_step(sn, okn, m_ref, l_ref, acc_ref, kvn[:, D_MODEL:2 * D_MODEL].astype(BF16))
        o = acc_ref[...] / l_ref[...]
        lam = _lambda_value(lam_ref, lam_init)
        for h in range(DIFF_HEADS):
            sl = slice(h * LANES, (h + 1) * LANES)
            o0 = o[(2 * h) * tn:(2 * h + 1) * tn, sl]
            o1 = o[(2 * h + 1) * tn:(2 * h + 2) * tn, sl]
            o_ref[:, sl] = _diff_finish(o0, o1, lam, sub_ref[...], lam_init)


def _diff_sample(q, cache, page_table, kv_new, lam_p, sub, *, tn, lam_init):
    n = q.shape[0]
    n_seq, n_pages = page_table.shape
    page = cache.shape[1]
    rows = 2 * DIFF_HEADS * tn
    grid_spec = pltpu.PrefetchScalarGridSpec(
        num_scalar_prefetch=1,
        grid=(n_seq, n_pages),
        in_specs=[pl.BlockSpec((tn, D_MODEL), lambda s, p, pt: (s, 0)),
                  pl.BlockSpec((1, page, 2 * D_MODEL), lambda s, p, pt: (pt[s * n_pages + p], 0, 0)),
                  pl.BlockSpec((tn, 2 * D_MODEL), lambda s, p, pt: (s, 0)),
                  pl.BlockSpec(lam_p.shape, lambda s, p, pt: (0, 0)),
                  pl.BlockSpec(sub.shape, lambda s, p, pt: (0, 0))],
        out_specs=pl.BlockSpec((tn, D_MODEL), lambda s, p, pt: (s, 0)),
        scratch_shapes=[pltpu.VMEM((rows, D_MODEL), BF16), pltpu.VMEM((rows, 1), F32),
                        pltpu.VMEM((rows, 1), F32), pltpu.VMEM((rows, D_MODEL), F32)],
    )
    return pl.pallas_call(
        functools.partial(_diff_sample_kernel, tn=tn, lam_init=lam_init),
        grid_spec=grid_spec,
        out_shape=jax.ShapeDtypeStruct((n, D_MODEL), F32),
        compiler_params=_params(2), name="diff_sample",
    )(page_table.reshape(-1), q, cache, kv_new, lam_p, sub)


def _chunk_layout(rows, n_seq):
    t_len = rows.shape[0] // n_seq
    x = rows.reshape(n_seq, t_len // CMP_STRIDE, CMP_STRIDE, 4, HEAD_DIM)
    return x.transpose(0, 3, 1, 2, 4).reshape(n_seq, 4, t_len // CMP_STRIDE, CMP_STRIDE * HEAD_DIM)


def _row_tile(n):
    return 256 if n % 256 == 0 else n


def kernel(x_prompt, x_sample, cache_nsa_cmp, cache_nsa_sel, state_nsa_win, cache_diff_kv, page_table,
           norm_mix, norm_mlp, norm_final, nsa_w_in, nsa_cmp_pe, nsa_cmp_w1, nsa_cmp_w2, nsa_w_out,
           diff_w_in, diff_lambda, diff_subnorm, diff_w_out, mlp_w_up, mlp_w_down):
    n_b, t_len, _ = x_prompt.shape
    n_db, tn, _ = x_sample.shape
    n_pages = page_table.shape[1]
    page = cache_nsa_cmp.shape[2]
    past = n_pages * page
    assert t_len % 512 == 0 and past % 512 == 0 and tn == 8 and state_nsa_win.shape[2] == WINDOW
    assert t_len // SEL_BLOCK <= LANES and past // SEL_BLOCK <= LANES
    np_rows, ns_rows = n_b * t_len, n_db * tn
    tm_p, tm_s = _row_tile(np_rows), _row_tile(ns_rows)

    hp = x_prompt.reshape(np_rows, D_MODEL)
    hs = x_sample.reshape(ns_rows, D_MODEL)
    pos_p = jnp.tile(jnp.arange(t_len), n_b)
    pos_s = jnp.tile(past + jnp.arange(tn), n_db)
    cos_p, sin_p = _rope_tables(pos_p)
    cos_s, sin_s = _rope_tables(pos_s)

    nq, nkv = NSA_HEADS * HEAD_DIM, 6 * NSA_KV_HEADS * HEAD_DIM
    w_in = nsa_w_in[0]
    wq = w_in[:, :nq].astype(BF16)
    wkv = w_in[:, nq:nq + nkv].astype(BF16)
    wg = jnp.pad(w_in[:, nq + nkv:], ((0, 0), (0, LANES - 3 * NSA_HEADS))).astype(BF16)
    g_mix0 = norm_mix[0][None]
    q_p, kv_p, kvb_p, gate_p = _nsa_proj(hp, g_mix0, wq, wkv, wg, cos_p, sin_p, tm_p)
    q_s, kv_s, _, gate_s = _nsa_proj(hs, g_mix0, wq, wkv, wg, cos_s, sin_s, tm_s)

    w1 = nsa_cmp_w1[0].astype(BF16)
    w2 = nsa_cmp_w2[0]
    zeros = jnp.zeros_like(w2)
    w2w = jnp.stack([jnp.concatenate([w2, zeros], axis=-1), jnp.concatenate([zeros, w2], axis=-1)],
                    axis=1).astype(BF16)
    pe = jnp.pad(nsa_cmp_pe[0].reshape(2, 1, CMP_BLOCK * HEAD_DIM), ((0, 0), (0, 15), (0, 0)))

    def cmp_tables(nc):
        cos_c, sin_c = _rope_tables(jnp.arange(nc) * CMP_STRIDE)
        return (jnp.stack([cos_c, jnp.ones_like(cos_c)]), jnp.stack([sin_c, jnp.zeros_like(sin_c)]))

    xc_p = _chunk_layout(kv_p[:, 0:2 * LANES], n_b)
    kvc_p = _compress(xc_p, w1, w2w, pe, *cmp_tables(t_len // CMP_STRIDE))
    past_cmp = cache_nsa_cmp[0][page_table].reshape(n_db * past, 2 * LANES)
    xc_s = _chunk_layout(past_cmp, n_db)
    kvc_s = _compress(xc_s, w1, w2w, pe, *cmp_tables(past // CMP_STRIDE))

    tq = 128
    ocmp_p, sel_p = _cmp_select(q_p, kvc_p, tq=tq, blocks_per_seq=t_len // tq,
                                pos0_of_block=lambda i: i * tq, n_blocks=t_len // SEL_BLOCK, n_pick=SEL_TOPK)
    ocmp_s, sel_s = _cmp_select(q_s, kvc_s, tq=tn, blocks_per_seq=1,
                                pos0_of_block=lambda i: past, n_blocks=past // SEL_BLOCK, n_pick=SEL_TOPK - 1)

    osel_p = _sel_prompt(q_p, kvb_p, sel_p, n_seq=n_b, t_len=t_len, tq=tq, tk=512, k_col=2, v_col=3)
    owin_p = _win_prompt(q_p, kvb_p, n_seq=n_b, t_len=t_len, tq=tq, k_col=4, v_col=5)
    cache_sel = cache_nsa_sel[0].reshape(cache_nsa_sel.shape[1], page, 2 * LANES)
    osel_s = _sel_sample(q_s, cache_sel, page_table, sel_s, kv_s, tn=tn, k_col=2, v_col=3)
    win_buf = state_nsa_win[0].reshape(n_db, WINDOW, 2 * LANES)
    owin_s = _win_sample(q_s, win_buf, kv_s, tn=tn, past=past, k_col=4, v_col=5)

    head = jnp.arange(D_MODEL) // HEAD_DIM
    expand = jnp.stack([(jnp.arange(LANES)[:, None] == head[None, :] * 3 + br) for br in range(3)]).astype(BF16)
    w_out = nsa_w_out[0].astype(BF16)
    hp = _nsa_out(hp, ocmp_p, osel_p, owin_p, gate_p, expand, w_out, tm_p)
    hs = _nsa_out(hs, ocmp_s, osel_s, owin_s, gate_s, expand, w_out, tm_s)

    wu, wd = mlp_w_up[0].astype(BF16), mlp_w_down[0].astype(BF16)
    hp = _mlp(hp, norm_mlp[0][None], wu, wd, norm_final[None], tm_p, False)
    hs = _mlp(hs, norm_mlp[0][None], wu, wd, norm_final[None], tm_s, False)

    lam_init = 0.8 - 0.6 * math.exp(-0.3 * 1)
    w_d = diff_w_in[0].astype(BF16)
    g_mix1 = norm_mix[1][None]
    qd_p, kvd_p, kvdb_p = _diff_proj(hp, g_mix1, w_d, cos_p, sin_p, tm_p)
    qd_s, kvd_s, _ = _diff_proj(hs, g_mix1, w_d, cos_s, sin_s, tm_s)
    lam_p = diff_lambda[0]
    sub = diff_subnorm[0][None]
    od_p = _diff_prompt(qd_p, kvdb_p, lam_p, sub, n_seq=n_b, t_len=t_len, tq=512, tk=512, lam_init=lam_init)
    cache_d = cache_diff_kv[0].reshape(cache_diff_kv.shape[1], page, 2 * D_MODEL)
    od_s = _diff_sample(qd_s, cache_d, page_table, kvd_s, lam_p, sub, tn=tn, lam_init=lam_init)
    w_do = diff_w_out[0].astype(BF16)
    hp = _res_proj(hp, od_p, w_do, tm_p)
    hs = _res_proj(hs, od_s, w_do, tm_s)

    wu, wd = mlp_w_up[1].astype(BF16), mlp_w_down[1].astype(BF16)
    y_p = _mlp(hp, norm_mlp[1][None], wu, wd, norm_final[None], tm_p, True)
    y_s = _mlp(hs, norm_mlp[1][None], wu, wd, norm_final[None], tm_s, True)

    hk, dh = NSA_KV_HEADS, HEAD_DIM
    kv_p4 = kv_p.reshape(n_b, t_len, 3, 2, hk, dh)
    kv_s4 = kv_s.reshape(n_db, tn, 3, 2, hk, dh)
    w_keep = min(WINDOW, t_len)
    win_s_new = jnp.concatenate([state_nsa_win[0][:, tn:], kv_s4[:, :, 2]], axis=1)
    return (y_p.reshape(n_b, t_len, D_MODEL), y_s.reshape(n_db, tn, D_MODEL),
            kv_p4[:, :, 0][None], kv_s4[:, :, 0][None],
            kv_p4[:, :, 1][None], kv_s4[:, :, 1][None],
            kv_p4[:, t_len - w_keep:, 2][None], win_s_new[None],
            kvd_p.reshape(1, n_b, t_len, 2, DIFF_HEADS, 2 * DIFF_DIM),
            kvd_s.reshape(1, n_db, tn, 2, DIFF_HEADS, 2 * DIFF_DIM))
```
